```python
import jax, jax.numpy as jnp
from jax import lax
import numpy as np

D_MODEL = 1024
BATCH = 8
SEQ = 2048
DEPTH = 1

D_MIX = D_MODEL
D_A = D_MIX // 2
D_B = D_MIX - D_A
HEAD_CH = 64
N_HEADS_A = D_A // HEAD_CH
N_HEADS_B = D_B // HEAD_CH
CONV_A_WIDTH = 3
CONV_B_WIDTH = 31
D_IN = 3 * D_A + 2 * D_B
N_GROUPS = 4
EXPERTS_PER_GROUP = 8
N_EXPERTS = N_GROUPS * EXPERTS_PER_GROUP
TOP_K_IN_GROUP = 2
D_EXPERT = D_MODEL // 2
EPS = 1e-6

kernel_name = "hybrid_shortconv_conformer_hmoe_encoder"


def rmsnorm(x, g):
    xf = x.astype(jnp.float32)
    r = lax.rsqrt(jnp.mean(xf * xf, axis=-1, keepdims=True) + EPS)
    return (xf * r).astype(x.dtype) * g


def layernorm(x, g, b):
    xf = x.astype(jnp.float32)
    mu = jnp.mean(xf, axis=-1, keepdims=True)
    var = jnp.mean(jnp.square(xf - mu), axis=-1, keepdims=True)
    return ((xf - mu) * lax.rsqrt(var + EPS)).astype(x.dtype) * g + b


def head_rmsnorm(y, n_heads, gain):
    bsz, s, c = y.shape
    yh = y.reshape(bsz, s, n_heads, c // n_heads).astype(jnp.float32)
    yh = yh * lax.rsqrt(jnp.mean(yh * yh, axis=-1, keepdims=True) + EPS)
    return yh.reshape(bsz, s, c).astype(y.dtype) * gain


def depthwise_conv_centred(u, w):
    k = w.shape[0]
    pad = (k - 1) // 2
    return lax.conv_general_dilated(
        u, w[:, None, :].astype(u.dtype), window_strides=(1,), padding=[(pad, pad)],
        dimension_numbers=("NWC", "WIO", "NWC"), feature_group_count=u.shape[-1])


def mixer_block(h, w_in, conv_a_w, conv_b_w, conv_b_bias, ln_b_g, ln_b_b, beta_a, beta_b, w_out):
    proj = jnp.einsum("bsd,dc->bsc", h, w_in)
    a_in, a_b, a_c, b_val, b_gate = jnp.split(
        proj, [D_A, 2 * D_A, 3 * D_A, 3 * D_A + D_B], axis=-1)
    y_a = a_b * depthwise_conv_centred(a_c * a_in, conv_a_w)
    u = b_val * jax.nn.sigmoid(b_gate)
    u = depthwise_conv_centred(u, conv_b_w) + conv_b_bias
    y_b = jax.nn.silu(layernorm(u, ln_b_g, ln_b_b))
    y = jnp.concatenate([head_rmsnorm(y_a, N_HEADS_A, beta_a),
                         head_rmsnorm(y_b, N_HEADS_B, beta_b)], axis=-1)
    return jnp.einsum("bsc,cd->bsd", y, w_out)


def hierarchical_moe(h, w_route_group, b_route_group, w_route_expert, b_route_expert, w1, w3, w2):
    bsz, s, d = h.shape
    t = h.reshape(bsz * s, d)
    p_group = jax.nn.softmax((t @ w_route_group + b_route_group).astype(jnp.float32), axis=-1)
    g_idx = jnp.argmax(p_group, axis=-1)
    p_sel = jnp.take_along_axis(p_group, g_idx[:, None], axis=1)[:, 0]
    le = (t @ w_route_expert + b_route_expert).astype(jnp.float32)
    le = le.reshape(-1, N_GROUPS, EXPERTS_PER_GROUP)
    le_sel = jnp.take_along_axis(le, g_idx[:, None, None], axis=1)[:, 0]
    p_exp = jax.nn.softmax(le_sel, axis=-1)
    top_v, top_i = lax.top_k(p_exp, TOP_K_IN_GROUP)
    top_w = top_v / jnp.sum(top_v, axis=-1, keepdims=True)
    eid = g_idx[:, None] * EXPERTS_PER_GROUP + top_i
    combine = p_sel[:, None] * jnp.einsum(
        "tk,tke->te", top_w, jax.nn.one_hot(eid, N_EXPERTS, dtype=jnp.float32))
    combine = combine.astype(t.dtype)
    out = jnp.zeros_like(t)
    for e in range(N_EXPERTS):
        act = jax.nn.silu(t @ w1[e]) * (t @ w3[e])
        out = out + combine[:, e:e + 1] * (act @ w2[e])
    return out.reshape(bsz, s, d)


def setup_inputs(seed: int = 0) -> dict:
    key = jax.random.key(seed)
    ks = jax.random.split(key, 20)
    f32 = jnp.float32
    nrm = lambda k, shape, scale: jax.random.normal(k, shape, f32) * scale
    L = DEPTH
    return {
        "x": jax.random.normal(ks[0], (BATCH, SEQ, D_MODEL), f32),
        "norm_mix_g": 1.0 + nrm(ks[1], (L, D_MODEL), 0.02),
        "w_in": nrm(ks[2], (L, D_MODEL, D_IN), D_MODEL ** -0.5),
        "conv_a_w": nrm(ks[3], (L, CONV_A_WIDTH, D_A), CONV_A_WIDTH ** -0.5),
        "conv_b_w": nrm(ks[4], (L, CONV_B_WIDTH, D_B), CONV_B_WIDTH ** -0.5),
        "conv_b_bias": nrm(ks[5], (L, D_B), 0.02),
        "ln_b_g": 1.0 + nrm(ks[6], (L, D_B), 0.02),
        "ln_b_b": nrm(ks[7], (L, D_B), 0.02),
        "beta_a": 1.0 + nrm(ks[8], (L, D_A), 0.02),
        "beta_b": 1.0 + nrm(ks[9], (L, D_B), 0.02),
        "w_out": nrm(ks[10], (L, D_MIX, D_MODEL), D_MIX ** -0.5),
        "norm_ffn_g": 1.0 + nrm(ks[11], (L, D_MODEL), 0.02),
        "w_route_group": nrm(ks[12], (L, D_MODEL, N_GROUPS), D_MODEL ** -0.5),
        "b_route_group": nrm(ks[13], (L, N_GROUPS), 0.01),
        "w_route_expert": nrm(ks[14], (L, D_MODEL, N_EXPERTS), D_MODEL ** -0.5),
        "b_route_expert": nrm(ks[15], (L, N_EXPERTS), 0.01),
        "w1": nrm(ks[16], (L, N_EXPERTS, D_MODEL, D_EXPERT), D_MODEL ** -0.5),
        "w3": nrm(ks[17], (L, N_EXPERTS, D_MODEL, D_EXPERT), D_MODEL ** -0.5),
        "w2": nrm(ks[18], (L, N_EXPERTS, D_EXPERT, D_MODEL), D_EXPERT ** -0.5),
        "norm_final_g": 1.0 + nrm(ks[19], (D_MODEL,), 0.02),
    }


def reference(x, norm_mix_g, w_in, conv_a_w, conv_b_w, conv_b_bias, ln_b_g, ln_b_b,
              beta_a, beta_b, w_out, norm_ffn_g, w_route_group, b_route_group,
              w_route_expert, b_route_expert, w1, w3, w2, norm_final_g):
    for l in range(DEPTH):
        h = rmsnorm(x, norm_mix_g[l])
        x = x + mixer_block(h, w_in[l], conv_a_w[l], conv_b_w[l], conv_b_bias[l],
                            ln_b_g[l], ln_b_b[l], beta_a[l], beta_b[l], w_out[l])
        h = rmsnorm(x, norm_ffn_g[l])
        x = x + hierarchical_moe(h, w_route_group[l], b_route_group[l], w_route_expert[l],
                                 b_route_expert[l], w1[l], w3[l], w2[l])
    return rmsnorm(x, norm_final_g)
```

```python
import functools

import jax
import jax.numpy as jnp
from jax import lax
from jax.experimental import pallas as pl
from jax.experimental.pallas import tpu as pltpu

EPS = 1e-6
HEAD_CH = 64
N_GROUPS = 4
EXPERTS_PER_GROUP = 8
N_EXPERTS = N_GROUPS * EXPERTS_PER_GROUP

LANES = 128
HALO = 16
SEQ_TILE = 512
CONV_ROWS = 64
ROW_TILE = 256
TOK_TILE = 256
VMEM_LIMIT = 56 * 1024 * 1024


def _rms(x, g):
    return x * lax.rsqrt(jnp.mean(x * x, axis=-1, keepdims=True) + EPS) * g


def _split_bf16(v):
    hi = v.astype(jnp.bfloat16)
    lo = (v - hi.astype(jnp.float32)).astype(jnp.bfloat16)
    return hi, lo


def _dot(a, b):
    return jnp.dot(a, b, preferred_element_type=jnp.float32)


def _mixer_kernel(xm_ref, xp_ref, xn_ref, gmix_ref, win_ref, caw_ref, cbw_ref, cbb_ref,
                  lng_ref, lnb_ref, beta_ref, wout_ref, gffn_ref, wr_ref, br_ref,
                  hsum_ref, hexp_ref,
                  x1_ref, h2_ref, mi_ref, mf_ref, cnt_ref,
                  hbuf, v_s, u_s, y_s, cnt_s, *, seq_len, d_a, d_b, ka, kb):
    ts = xm_ref.shape[1]
    tsh = ts + 2 * HALO
    j = pl.program_id(1)

    @pl.when((pl.program_id(0) == 0) & (j == 0))
    def _():
        cnt_s[...] = jnp.zeros_like(cnt_s)

    gmix = gmix_ref[...]
    hbuf[0:HALO, :] = _rms(xp_ref[0], gmix).astype(jnp.bfloat16)
    hbuf[HALO:HALO + ts, :] = _rms(xm_ref[0], gmix).astype(jnp.bfloat16)
    hbuf[HALO + ts:tsh, :] = _rms(xn_ref[0], gmix).astype(jnp.bfloat16)

    pos = j * ts - HALO + lax.broadcasted_iota(jnp.int32, (tsh, 1), 0)
    valid = (pos >= 0) & (pos < seq_len)

    h = hbuf[...]
    a_in = _dot(h, win_ref[:, 0:d_a])
    a_c = _dot(h, win_ref[:, 2 * d_a:3 * d_a])
    v_s[...] = jnp.where(valid, a_c * a_in, 0.0)
    b_val = _dot(h, win_ref[:, 3 * d_a:3 * d_a + d_b])
    b_gate = _dot(h, win_ref[:, 3 * d_a + d_b:3 * d_a + 2 * d_b])
    u_s[...] = jnp.where(valid, b_val * jax.nn.sigmoid(b_gate), 0.0)
    y_s[:, 0:d_a] = _dot(hbuf[HALO:HALO + ts, :], win_ref[:, d_a:2 * d_a])

    pad_a = (ka - 1) // 2
    pad_b = (kb - 1) // 2
    for r0 in range(0, ts, CONV_ROWS):
        acc = caw_ref[0:1, :] * v_s[HALO + r0 - pad_a:HALO + r0 - pad_a + CONV_ROWS, :]
        for k in range(1, ka):
            s = HALO + r0 + k - pad_a
            acc = acc + caw_ref[k:k + 1, :] * v_s[s:s + CONV_ROWS, :]
        y_s[r0:r0 + CONV_ROWS, 0:d_a] = y_s[r0:r0 + CONV_ROWS, 0:d_a] * acc
        acc = cbb_ref[...] + cbw_ref[0:1, :] * u_s[HALO + r0 - pad_b:HALO + r0 - pad_b + CONV_ROWS, :]
        for k in range(1, kb):
            s = HALO + r0 + k - pad_b
            acc = acc + cbw_ref[k:k + 1, :] * u_s[s:s + CONV_ROWS, :]
        mu = jnp.mean(acc, axis=-1, keepdims=True)
        cen = acc - mu
        var = jnp.mean(cen * cen, axis=-1, keepdims=True)
        z = cen * lax.rsqrt(var + EPS) * lng_ref[...] + lnb_ref[...]
        y_s[r0:r0 + CONV_ROWS, d_a:d_a + d_b] = z * jax.nn.sigmoid(z)

    y = y_s[...]
    sq_hi, sq_lo = _split_bf16(y * y)
    ssq = _dot(sq_hi, hsum_ref[...]) + _dot(sq_lo, hsum_ref[...])
    rs = lax.rsqrt(ssq * (1.0 / HEAD_CH) + EPS)
    rs_hi, rs_lo = _split_bf16(rs)
    scale = _dot(rs_hi, hexp_ref[...]) + _dot(rs_lo, hexp_ref[...])
    yn = (y * scale * beta_ref[...]).astype(jnp.bfloat16)

    x1 = xm_ref[0] + _dot(yn, wout_ref[...])
    x1_ref[0] = x1
    h2 = _rms(x1, gffn_ref[...])
    h2_ref[...] = h2

    logits = _dot(h2.astype(jnp.bfloat16), wr_ref[...]) + br_ref[...]
    lane = lax.broadcasted_iota(jnp.int32, (ts, LANES), 1)
    neg_inf = -jnp.inf
    gl = jnp.where(lane < N_GROUPS, logits, neg_inf)
    gmax = jnp.max(gl, axis=-1, keepdims=True)
    g_idx = jnp.min(jnp.where(gl == gmax, lane, LANES), axis=-1, keepdims=True)
    p_sel = 1.0 / jnp.sum(jnp.exp(gl - gmax), axis=-1, keepdims=True)
    e_lane = lane - N_GROUPS
    in_grp = (e_lane >= g_idx * EXPERTS_PER_GROUP) & (e_lane < (g_idx + 1) * EXPERTS_PER_GROUP)
    el = jnp.where(in_grp, logits, neg_inf)
    m0 = jnp.max(el, axis=-1, keepdims=True)
    i0 = jnp.min(jnp.where(el == m0, lane, LANES), axis=-1, keepdims=True)
    el1 = jnp.where(lane == i0, neg_inf, el)
    m1 = jnp.max(el1, axis=-1, keepdims=True)
    i1 = jnp.min(jnp.where(el1 == m1, lane, LANES), axis=-1, keepdims=True)
    ratio = jnp.exp(m1 - m0)
    c0 = p_sel / (1.0 + ratio)
    c1 = p_sel * ratio / (1.0 + ratio)

    hot0 = lane == i0
    hot1 = lane == i1
    assign = jnp.where(hot0 | hot1, 1.0, 0.0)
    tri = (lax.broadcasted_iota(jnp.int32, (ts, ts), 1)
           < lax.broadcasted_iota(jnp.int32, (ts, ts), 0))
    before = _dot(jnp.where(tri, 1.0, 0.0).astype(jnp.bfloat16), assign.astype(jnp.bfloat16))
    before = before + cnt_s[...]
    rank0 = jnp.sum(jnp.where(hot0, before, 0.0), axis=-1, keepdims=True).astype(jnp.int32)
    rank1 = jnp.sum(jnp.where(hot1, before, 0.0), axis=-1, keepdims=True).astype(jnp.int32)
    cnt_s[...] = cnt_s[...] + jnp.sum(assign, axis=0, keepdims=True)
    cnt_ref[...] = cnt_s[...]

    mi = jnp.where(lane == 0, i0 - N_GROUPS, 0)
    mi = jnp.where(lane == 1, i1 - N_GROUPS, mi)
    mi = jnp.where(lane == 2, rank0, mi)
    mi = jnp.where(lane == 3, rank1, mi)
    mi_ref[...] = mi
    mf_ref[...] = jnp.where(lane == 0, c0, jnp.where(lane == 1, c1, 0.0))


def _mixer(x, gmix, win, caw, cbw, cbb, lng, lnb, beta, wout, gffn, wr, br, hsum, hexp):
    bsz, seq, d = x.shape
    d_a = caw.shape[1]
    d_b = cbw.shape[1]
    ts = SEQ_TILE
    nt = seq // ts
    tsh = ts + 2 * HALO
    t = bsz * seq
    hb = ts // HALO

    def full(a):
        return pl.BlockSpec(a.shape, lambda b, j: (0,) * a.ndim)

    kern = functools.partial(_mixer_kernel, seq_len=seq, d_a=d_a, d_b=d_b,
                             ka=caw.shape[0], kb=cbw.shape[0])
    params = (gmix, win, caw, cbw, cbb, lng, lnb, beta, wout, gffn, wr, br, hsum, hexp)
    return pl.pallas_call(
        kern,
        grid=(bsz, nt),
        in_specs=[
            pl.BlockSpec((1, ts, d), lambda b, j: (b, j, 0)),
            pl.BlockSpec((1, HALO, d), lambda b, j: (b, jnp.maximum(j * hb - 1, 0), 0)),
            pl.BlockSpec((1, HALO, d), lambda b, j: (b, jnp.minimum((j + 1) * hb, seq // HALO - 1), 0)),
        ] + [full(a) for a in params],
        out_specs=[
            pl.BlockSpec((1, ts, d), lambda b, j: (b, j, 0)),
            pl.BlockSpec((ts, d), lambda b, j: (b * nt + j, 0)),
            pl.BlockSpec((ts, LANES), lambda b, j: (b * nt + j, 0)),
            pl.BlockSpec((ts, LANES), lambda b, j: (b * nt + j, 0)),
            pl.BlockSpec((1, LANES), lambda b, j: (0, 0)),
        ],
        out_shape=[
            jax.ShapeDtypeStruct((bsz, seq, d), jnp.float32),
            jax.ShapeDtypeStruct((t, d), jnp.float32),
            jax.ShapeDtypeStruct((t, LANES), jnp.int32),
            jax.ShapeDtypeStruct((t, LANES), jnp.float32),
            jax.ShapeDtypeStruct((1, LANES), jnp.float32),
        ],
        scratch_shapes=[
            pltpu.VMEM((tsh, d), jnp.bfloat16),
            pltpu.VMEM((tsh, d_a), jnp.float32),
            pltpu.VMEM((tsh, d_b), jnp.float32),
            pltpu.VMEM((ts, d_a + d_b), jnp.float32),
            pltpu.VMEM((1, LANES), jnp.float32),
        ],
        compiler_params=pltpu.CompilerParams(
            dimension_semantics=("arbitrary", "arbitrary"), vmem_limit_bytes=VMEM_LIMIT),
        name="mixer_router",
    )(x, x, x, *params)


def _dispatch_kernel(pos_ref, h_ref, xg_ref, sem):
    tt = h_ref.shape[0]

    def row_copy(i, k):
        return pltpu.make_async_copy(
            h_ref.at[pl.ds(i, 1)], xg_ref.at[pl.ds(pos_ref[0, 0, 2 * i + k], 1)], sem)

    def start(i, c):
        row_copy(i, 0).start()
        row_copy(i, 1).start()
        return c

    def wait(i, c):
        row_copy(i, 0).wait()
        row_copy(i, 1).wait()
        return c

    lax.fori_loop(0, tt, start, 0)
    lax.fori_loop(0, tt, wait, 0)


def _dispatch(pos, h2):
    t, d = h2.shape
    tt = TOK_TILE
    pos3 = pos.reshape(t // tt, 1, 2 * tt)
    return pl.pallas_call(
        _dispatch_kernel,
        grid=(t // tt,),
        in_specs=[
            pl.BlockSpec((1, 1, 2 * tt), lambda i: (i, 0, 0), memory_space=pltpu.SMEM),
            pl.BlockSpec((tt, d), lambda i: (i, 0)),
        ],
        out_specs=pl.BlockSpec(memory_space=pl.ANY),
        out_shape=jax.ShapeDtypeStruct((2 * t, d), jnp.float32),
        scratch_shapes=[pltpu.SemaphoreType.DMA(())],
        compiler_params=pltpu.CompilerParams(dimension_semantics=("arbitrary",)),
        name="dispatch",
    )(pos3, h2)


def _expert_kernel(tile_ref, exp_ref, lo_ref, hi_ref, newexp_ref, newtile_ref, nq_ref,
                   xg_ref, w1_ref, w3_ref, w2_ref, y_ref, w1b, w3b, w2b):
    q = pl.program_id(0)

    @pl.when(q < nq_ref[0])
    def _():
        @pl.when(newexp_ref[q] == 1)
        def _():
            w1b[...] = w1_ref[0].astype(jnp.bfloat16)
            w3b[...] = w3_ref[0].astype(jnp.bfloat16)
            w2b[...] = w2_ref[0].astype(jnp.bfloat16)

        x = xg_ref[...].astype(jnp.bfloat16)
        act = jax.nn.silu(_dot(x, w1b[...])) * _dot(x, w3b[...])
        y = _dot(act.astype(jnp.bfloat16), w2b[...])
        row = lax.broadcasted_iota(jnp.int32, (y.shape[0], 1), 0)
        mine = (row >= lo_ref[q]) & (row < hi_ref[q])

        @pl.when(newtile_ref[q] == 1)
        def _():
            y_ref[...] = jnp.where(mine, y, 0.0)

        @pl.when(newtile_ref[q] == 0)
        def _():
            y_ref[...] = jnp.where(mine, y, y_ref[...])


def _experts(plan, xg, w1, w3, w2):
    p, d = xg.shape
    n_e, _, d_e = w1.shape
    rt = ROW_TILE
    n_steps = p // rt + n_e - 1
    grid_spec = pltpu.PrefetchScalarGridSpec(
        num_scalar_prefetch=7,
        grid=(n_steps,),
        in_specs=[
            pl.BlockSpec((rt, d), lambda q, tile, exp, *_: (tile[q], 0)),
            pl.BlockSpec((1, d, d_e), lambda q, tile, exp, *_: (exp[q], 0, 0)),
            pl.BlockSpec((1, d, d_e), lambda q, tile, exp, *_: (exp[q], 0, 0)),
            pl.BlockSpec((1, d_e, d), lambda q, tile, exp, *_: (exp[q], 0, 0)),
        ],
        out_specs=pl.BlockSpec((rt, d), lambda q, tile, exp, *_: (tile[q], 0)),
        scratch_shapes=[
            pltpu.VMEM((d, d_e), jnp.bfloat16),
            pltpu.VMEM((d, d_e), jnp.bfloat16),
            pltpu.VMEM((d_e, d), jnp.bfloat16),
        ],
    )
    return pl.pallas_call(
        _expert_kernel,
        grid_spec=grid_spec,
        out_shape=jax.ShapeDtypeStruct((p, d), jnp.float32),
        compiler_params=pltpu.CompilerParams(
            dimension_semantics=("arbitrary",), vmem_limit_bytes=VMEM_LIMIT),
        name="experts",
    )(*plan, xg, w1, w3, w2)


def _expert_plan(counts, n_rows):
    n_e = counts.shape[0]
    rt = ROW_TILE
    n_steps = n_rows // rt + n_e - 1
    end = jnp.cumsum(counts)
    off = end - counts
    first_tile = off // rt
    n_tiles = jnp.where(counts > 0, (end - 1) // rt - first_tile + 1, 0)
    q_end = jnp.cumsum(n_tiles)
    q_off = q_end - n_tiles
    nq = q_end[-1]
    q = jnp.minimum(jnp.arange(n_steps, dtype=jnp.int32), nq - 1)
    exp = jnp.sum((q[:, None] >= q_end[None, :]).astype(jnp.int32), axis=1)
    exp = jnp.minimum(exp, n_e - 1)
    tile = first_tile[exp] + q - q_off[exp]
    lo = jnp.maximum(off[exp] - tile * rt, 0)
    hi = jnp.minimum(end[exp] - tile * rt, rt)
    newexp = (q == q_off[exp]).astype(jnp.int32)
    newtile = jnp.concatenate([jnp.ones((1,), jnp.int32),
                               (tile[1:] != tile[:-1]).astype(jnp.int32)])
    to_i32 = lambda a: a.astype(jnp.int32)
    return (to_i32(tile), to_i32(exp), to_i32(lo), to_i32(hi), newexp, newtile,
            to_i32(nq).reshape(1)), off


def _combine_kernel(pos_ref, x1_ref, mf_ref, gfin_ref, y_ref, out_ref, ybuf, sem, *, final_norm):
    tt = x1_ref.shape[0]

    def row_copy(i, k):
        return pltpu.make_async_copy(
            y_ref.at[pl.ds(pos_ref[0, 0, 2 * i + k], 1)], ybuf.at[k, pl.ds(i, 1)], sem)

    def start(i, c):
        row_copy(i, 0).start()
        row_copy(i, 1).start()
        return c

    def wait(i, c):
        row_copy(i, 0).wait()
        row_copy(i, 1).wait()
        return c

    lax.fori_loop(0, tt, start, 0)
    lax.fori_loop(0, tt, wait, 0)
    mf = mf_ref[...]
    x2 = x1_ref[...] + mf[:, 0:1] * ybuf[0] + mf[:, 1:2] * ybuf[1]
    out_ref[...] = _rms(x2, gfin_ref[...]) if final_norm else x2


def _combine(pos, x1, mf, gfin, y, final_norm):
    t, d = x1.shape
    tt = TOK_TILE
    pos3 = pos.reshape(t // tt, 1, 2 * tt)
    return pl.pallas_call(
        functools.partial(_combine_kernel, final_norm=final_norm),
        grid=(t // tt,),
        in_specs=[
            pl.BlockSpec((1, 1, 2 * tt), lambda i: (i, 0, 0), memory_space=pltpu.SMEM),
            pl.BlockSpec((tt, d), lambda i: (i, 0)),
            pl.BlockSpec((tt, LANES), lambda i: (i, 0)),
            pl.BlockSpec((1, d), lambda i: (0, 0)),
            pl.BlockSpec(memory_space=pl.ANY),
        ],
        out_specs=pl.BlockSpec((tt, d), lambda i: (i, 0)),
        out_shape=jax.ShapeDtypeStruct((t, d), jnp.float32),
        scratch_shapes=[pltpu.VMEM((2, tt, d), jnp.float32), pltpu.SemaphoreType.DMA(())],
        compiler_params=pltpu.CompilerParams(dimension_semantics=("arbitrary",)),
        name="combine",
    )(pos3, x1, mf, gfin, y)


def _layer(x, gmix, w_in, caw, cbw, cbb, lng, lnb, beta_a, beta_b, w_out, gffn,
           w_rg, b_rg, w_re, b_re, w1, w3, w2):
    bsz, seq, d = x.shape
    t = bsz * seq
    d_mix = w_out.shape[0]
    bf = jnp.bfloat16
    n_route = N_GROUPS + N_EXPERTS
    wr = jnp.pad(jnp.concatenate([w_rg, w_re], axis=1), ((0, 0), (0, LANES - n_route))).astype(bf)
    br = jnp.pad(jnp.concatenate([b_rg, b_re]), (0, LANES - n_route)).reshape(1, LANES)
    head_of = jnp.arange(d_mix, dtype=jnp.int32) // HEAD_CH
    hsum = (head_of[:, None] == jnp.arange(LANES, dtype=jnp.int32)[None, :]).astype(bf)
    hexp = hsum.T
    row = lambda a: a.reshape(1, -1)

    x1, h2, mi, mf, cnt = _mixer(
        x, row(gmix), w_in.astype(bf), caw, cbw, row(cbb), row(lng), row(lnb),
        row(jnp.concatenate([beta_a, beta_b])), w_out.astype(bf), row(gffn), wr, br, hsum, hexp)

    counts = cnt[0, N_GROUPS:N_GROUPS + N_EXPERTS].astype(jnp.int32)
    plan, off = _expert_plan(counts, 2 * t)
    pos = (off[mi[:, 0:2]] + mi[:, 2:4]).astype(jnp.int32)

    xg = _dispatch(pos, h2)
    y = _experts(plan, xg, w1, w3, w2)
    return x1.reshape(t, d), pos, mf, y


def kernel(x, norm_mix_g, w_in, conv_a_w, conv_b_w, conv_b_bias, ln_b_g, ln_b_b, beta_a, beta_b,
           w_out, norm_ffn_g, w_route_group, b_route_group, w_route_expert, b_route_expert,
           w1, w3, w2, norm_final_g):
    depth = w_in.shape[0]
    bsz, seq, d = x.shape
    gfin = norm_final_g.reshape(1, d)
    for l in range(depth):
        x1, pos, mf, y = _layer(
            x, norm_mix_g[l], w_in[l], conv_a_w[l], conv_b_w[l], conv_b_bias[l], ln_b_g[l],
            ln_b_b[l], beta_a[l], beta_b[l], w_out[l], norm_ffn_g[l], w_route_group[l],
            b_route_group[l], w_route_expert[l], b_route_expert[l], w1[l], w3[l], w2[l])
        out = _combine(pos, x1, mf, gfin, y, final_norm=(l == depth - 1))
        x = out.reshape(bsz, seq, d)
    return x
```

```python
import functools

import jax
import jax.numpy as jnp
from jax import lax
from jax.experimental import pallas as pl
from jax.experimental.pallas import tpu as pltpu

EPS = 1e-6
HEAD_CH = 64
N_GROUPS = 4
EXPERTS_PER_GROUP = 8
N_EXPERTS = N_GROUPS * EXPERTS_PER_GROUP

LANES = 128
SUBLANES = 8
HALO = 16
SEQ_TILE = 512
CONV_ROWS = 64
ROW_TILE = 256
DISPATCH_TILE = 512
DISPATCH_BUFS = 3
COMBINE_TILE = 256
ISSUE_UNROLL = 8
META_ROWS = 8
VMEM_LIMIT = 56 * 1024 * 1024


def _rms(x, g):
    return x * lax.rsqrt(jnp.mean(x * x, axis=-1, keepdims=True) + EPS) * g


def _split_bf16(v):
    hi = v.astype(jnp.bfloat16)
    lo = (v - hi.astype(jnp.float32)).astype(jnp.bfloat16)
    return hi, lo


def _dot(a, b):
    return jnp.dot(a, b, preferred_element_type=jnp.float32)


def _mixer_kernel(xm_ref, xp_ref, xn_ref, gmix_ref, win_ref, caw_ref, cbw_ref, cbb_ref,
                  lng_ref, lnb_ref, beta_ref, wout_ref, gffn_ref, wr_ref, br_ref,
                  hsum_ref, hexp_ref,
                  x1_ref, h2p_ref, meta_ref, mf_ref, cnt_ref,
                  hbuf, v_s, u_sh, y_s, cnt_s, *, seq_len, d_a, d_b, ka, kb):
    ts = xm_ref.shape[1]
    tsh = ts + 2 * HALO
    n_sh = tsh - SUBLANES
    j = pl.program_id(1)

    @pl.when((pl.program_id(0) == 0) & (j == 0))
    def _():
        cnt_s[...] = jnp.zeros_like(cnt_s)

    gmix = gmix_ref[...]
    hbuf[0:HALO, :] = _rms(xp_ref[0], gmix).astype(jnp.bfloat16)
    hbuf[HALO:HALO + ts, :] = _rms(xm_ref[0], gmix).astype(jnp.bfloat16)
    hbuf[HALO + ts:tsh, :] = _rms(xn_ref[0], gmix).astype(jnp.bfloat16)

    pos = j * ts - HALO + lax.broadcasted_iota(jnp.int32, (tsh, 1), 0)
    valid = (pos >= 0) & (pos < seq_len)

    h = hbuf[...]
    a_in = _dot(h, win_ref[:, 0:d_a])
    a_c = _dot(h, win_ref[:, 2 * d_a:3 * d_a])
    v_s[...] = jnp.where(valid, a_c * a_in, 0.0)
    b_val = _dot(h, win_ref[:, 3 * d_a:3 * d_a + d_b])
    b_gate = _dot(h, win_ref[:, 3 * d_a + d_b:3 * d_a + 2 * d_b])
    u_sh[0] = jnp.where(valid, b_val * jax.nn.sigmoid(b_gate), 0.0)
    for r in range(1, SUBLANES):
        u_sh[r, 0:n_sh, :] = u_sh[0, r:r + n_sh, :]
    y_s[:, 0:d_a] = _dot(hbuf[HALO:HALO + ts, :], win_ref[:, d_a:2 * d_a])

    pad_a = (ka - 1) // 2
    pad_b = (kb - 1) // 2
    for r0 in range(0, ts, CONV_ROWS):
        acc = caw_ref[0:1, :] * v_s[HALO + r0 - pad_a:HALO + r0 - pad_a + CONV_ROWS, :]
        for k in range(1, ka):
            s = HALO + r0 + k - pad_a
            acc = acc + caw_ref[k:k + 1, :] * v_s[s:s + CONV_ROWS, :]
        y_s[r0:r0 + CONV_ROWS, 0:d_a] = y_s[r0:r0 + CONV_ROWS, 0:d_a] * acc
        acc = cbb_ref[...]
        for k in range(kb):
            s = HALO + r0 + k - pad_b
            base = s - s % SUBLANES
            acc = acc + cbw_ref[k:k + 1, :] * u_sh[s % SUBLANES, base:base + CONV_ROWS, :]
        mu = jnp.mean(acc, axis=-1, keepdims=True)
        cen = acc - mu
        var = jnp.mean(cen * cen, axis=-1, keepdims=True)
        z = cen * lax.rsqrt(var + EPS) * lng_ref[...] + lnb_ref[...]
        y_s[r0:r0 + CONV_ROWS, d_a:d_a + d_b] = z * jax.nn.sigmoid(z)

    y = y_s[...]
    sq_hi, sq_lo = _split_bf16(y * y)
    ssq = _dot(sq_hi, hsum_ref[...]) + _dot(sq_lo, hsum_ref[...])
    rs = lax.rsqrt(ssq * (1.0 / HEAD_CH) + EPS)
    rs_hi, rs_lo = _split_bf16(rs)
    scale = _dot(rs_hi, hexp_ref[...]) + _dot(rs_lo, hexp_ref[...])
    yn = (y * scale * beta_ref[...]).astype(jnp.bfloat16)

    x1 = xm_ref[0] + _dot(yn, wout_ref[...])
    x1_ref[0] = x1
    h2 = _rms(x1, gffn_ref[...])
    d = h2.shape[1]
    h2p_ref[...] = pltpu.pack_elementwise([h2[:, 0:d // 2], h2[:, d // 2:d]],
                                          packed_dtype=jnp.bfloat16)

    logits = _dot(h2.astype(jnp.bfloat16), wr_ref[...]) + br_ref[...]
    lane = lax.broadcasted_iota(jnp.int32, (ts, LANES), 1)
    neg_inf = -jnp.inf
    gl = jnp.where(lane < N_GROUPS, logits, neg_inf)
    gmax = jnp.max(gl, axis=-1, keepdims=True)
    g_idx = jnp.min(jnp.where(gl == gmax, lane, LANES), axis=-1, keepdims=True)
    p_sel = 1.0 / jnp.sum(jnp.exp(gl - gmax), axis=-1, keepdims=True)
    e_lane = lane - N_GROUPS
    in_grp = (e_lane >= g_idx * EXPERTS_PER_GROUP) & (e_lane < (g_idx + 1) * EXPERTS_PER_GROUP)
    el = jnp.where(in_grp, logits, neg_inf)
    m0 = jnp.max(el, axis=-1, keepdims=True)
    i0 = jnp.min(jnp.where(el == m0, lane, LANES), axis=-1, keepdims=True)
    el1 = jnp.where(lane == i0, neg_inf, el)
    m1 = jnp.max(el1, axis=-1, keepdims=True)
    i1 = jnp.min(jnp.where(el1 == m1, lane, LANES), axis=-1, keepdims=True)
    ratio = jnp.exp(m1 - m0)
    c0 = p_sel / (1.0 + ratio)
    c1 = p_sel * ratio / (1.0 + ratio)

    hot0 = lane == i0
    hot1 = lane == i1
    assign = jnp.where(hot0 | hot1, 1.0, 0.0)
    tri = (lax.broadcasted_iota(jnp.int32, (ts, ts), 1)
           < lax.broadcasted_iota(jnp.int32, (ts, ts), 0))
    before = _dot(jnp.where(tri, 1.0, 0.0).astype(jnp.bfloat16), assign.astype(jnp.bfloat16))
    before = before + cnt_s[...]
    rank0 = jnp.sum(jnp.where(hot0, before, 0.0), axis=-1, keepdims=True)
    rank1 = jnp.sum(jnp.where(hot1, before, 0.0), axis=-1, keepdims=True)
    cnt_s[...] = cnt_s[...] + jnp.sum(assign, axis=0, keepdims=True)
    cnt_ref[...] = cnt_s[...]

    rec = jnp.where(lane == 0, (i0 - N_GROUPS).astype(jnp.float32), 0.0)
    rec = jnp.where(lane == 1, (i1 - N_GROUPS).astype(jnp.float32), rec)
    rec = jnp.where(lane == 2, rank0, rec)
    rec = jnp.where(lane == 3, rank1, rec)
    meta_ref[0] = rec.T[0:META_ROWS, :]
    mf_ref[...] = jnp.where(lane == 0, c0, jnp.where(lane == 1, c1, 0.0))


def _mixer(x, gmix, win, caw, cbw, cbb, lng, lnb, beta, wout, gffn, wr, br, hsum, hexp):
    bsz, seq, d = x.shape
    d_a = caw.shape[1]
    d_b = cbw.shape[1]
    ts = SEQ_TILE
    nt = seq // ts
    tsh = ts + 2 * HALO
    t = bsz * seq
    hb = ts // HALO

    def full(a):
        return pl.BlockSpec(a.shape, lambda b, j: (0,) * a.ndim)

    kern = functools.partial(_mixer_kernel, seq_len=seq, d_a=d_a, d_b=d_b,
                             ka=caw.shape[0], kb=cbw.shape[0])
    params = (gmix, win, caw, cbw, cbb, lng, lnb, beta, wout, gffn, wr, br, hsum, hexp)
    return pl.pallas_call(
        kern,
        grid=(bsz, nt),
        in_specs=[
            pl.BlockSpec((1, ts, d), lambda b, j: (b, j, 0)),
            pl.BlockSpec((1, HALO, d), lambda b, j: (b, jnp.maximum(j * hb - 1, 0), 0)),
            pl.BlockSpec((1, HALO, d), lambda b, j: (b, jnp.minimum((j + 1) * hb, seq // HALO - 1), 0)),
        ] + [full(a) for a in params],
        out_specs=[
            pl.BlockSpec((1, ts, d), lambda b, j: (b, j, 0)),
            pl.BlockSpec((ts, d // 2), lambda b, j: (b * nt + j, 0)),
            pl.BlockSpec((1, META_ROWS, ts), lambda b, j: (b * nt + j, 0, 0)),
            pl.BlockSpec((ts, LANES), lambda b, j: (b * nt + j, 0)),
            pl.BlockSpec((1, LANES), lambda b, j: (0, 0)),
        ],
        out_shape=[
            jax.ShapeDtypeStruct((bsz, seq, d), jnp.float32),
            jax.ShapeDtypeStruct((t, d // 2), jnp.uint32),
            jax.ShapeDtypeStruct((t // ts, META_ROWS, ts), jnp.float32),
            jax.ShapeDtypeStruct((t, LANES), jnp.float32),
            jax.ShapeDtypeStruct((1, LANES), jnp.float32),
        ],
        scratch_shapes=[
            pltpu.VMEM((tsh, d), jnp.bfloat16),
            pltpu.VMEM((tsh, d_a), jnp.float32),
            pltpu.VMEM((SUBLANES, tsh, d_b), jnp.float32),
            pltpu.VMEM((ts, d_a + d_b), jnp.float32),
            pltpu.VMEM((1, LANES), jnp.float32),
        ],
        compiler_params=pltpu.CompilerParams(
            dimension_semantics=("arbitrary", "arbitrary"), vmem_limit_bytes=VMEM_LIMIT),
        name="mixer_router",
    )(x, x, x, *params)


def _dispatch_kernel(pos_ref, h_hbm, xg_ref, hbuf, in_sem, out_sem):
    i = pl.program_id(0)
    n = pl.num_programs(0)
    nb, tt, _ = hbuf.shape
    slot = i % nb

    def in_copy(step, s):
        return pltpu.make_async_copy(h_hbm.at[pl.ds(step * tt, tt)], hbuf.at[s], in_sem.at[s])

    def drain(s):
        pltpu.make_async_copy(xg_ref.at[pl.ds(0, 2 * tt)], xg_ref.at[pl.ds(0, 2 * tt)],
                              out_sem.at[s]).wait()

    @pl.when(i == 0)
    def _():
        in_copy(0, 0).start()

    in_copy(i, slot).wait()

    @pl.when(i >= nb - 1)
    def _():
        drain((i + 1) % nb)

    @pl.when(i + 1 < n)
    def _():
        in_copy(i + 1, (i + 1) % nb).start()

    def issue(g, c):
        for u in range(ISSUE_UNROLL):
            r = g * ISSUE_UNROLL + u
            for k in range(2):
                pltpu.make_async_copy(hbuf.at[slot, pl.ds(r, 1)],
                                      xg_ref.at[pl.ds(pos_ref[0, 0, 2 * r + k], 1)],
                                      out_sem.at[slot]).start()
        return c

    lax.fori_loop(0, tt // ISSUE_UNROLL, issue, 0)

    @pl.when(i == n - 1)
    def _():
        for back in range(nb - 2, -1, -1):
            @pl.when(i >= back)
            def _():
                drain((i - back) % nb)


def _dispatch(pos, h2p):
    t, dw = h2p.shape
    tt = DISPATCH_TILE
    pos3 = pos.reshape(t // tt, 1, 2 * tt)
    return pl.pallas_call(
        _dispatch_kernel,
        grid=(t // tt,),
        in_specs=[
            pl.BlockSpec((1, 1, 2 * tt), lambda i: (i, 0, 0), memory_space=pltpu.SMEM),
            pl.BlockSpec(memory_space=pl.ANY),
        ],
        out_specs=pl.BlockSpec(memory_space=pl.ANY),
        out_shape=jax.ShapeDtypeStruct((2 * t, dw), h2p.dtype),
        scratch_shapes=[
            pltpu.VMEM((DISPATCH_BUFS, tt, dw), h2p.dtype),
            pltpu.SemaphoreType.DMA((DISPATCH_BUFS,)),
            pltpu.SemaphoreType.DMA((DISPATCH_BUFS,)),
        ],
        compiler_params=pltpu.CompilerParams(dimension_semantics=("arbitrary",)),
        name="dispatch",
    )(pos3, h2p)


def _expert_kernel(tile_ref, exp_ref, lo_ref, hi_ref, newexp_ref, newtile_ref, nextexp_ref,
                   hasnext_ref, nq_ref,
                   xg_ref, w1_hbm, w3_hbm, w2_hbm, y_ref,
                   w1s, w3s, w2s, w1b, w3b, w2b, wsem):
    q = pl.program_id(0)

    def weight_copies(e):
        return (pltpu.make_async_copy(w1_hbm.at[e], w1s, wsem.at[0]),
                pltpu.make_async_copy(w3_hbm.at[e], w3s, wsem.at[1]),
                pltpu.make_async_copy(w2_hbm.at[e], w2s, wsem.at[2]))

    @pl.when(q == 0)
    def _():
        for c in weight_copies(exp_ref[0]):
            c.start()

    @pl.when(q < nq_ref[0])
    def _():
        @pl.when(newexp_ref[q] == 1)
        def _():
            for c in weight_copies(exp_ref[q]):
                c.wait()
            w1b[...] = w1s[...].astype(jnp.bfloat16)
            w3b[...] = w3s[...].astype(jnp.bfloat16)
            w2b[...] = w2s[...].astype(jnp.bfloat16)

            @pl.when(hasnext_ref[q] == 1)
            def _():
                for c in weight_copies(nextexp_ref[q]):
                    c.start()

        xu = xg_ref[...]
        x = jnp.concatenate(
            [pltpu.unpack_elementwise(xu, index=k, packed_dtype=jnp.bfloat16,
                                      unpacked_dtype=jnp.float32).astype(jnp.bfloat16)
             for k in range(2)], axis=1)
        act = jax.nn.silu(_dot(x, w1b[...])) * _dot(x, w3b[...])
        y = _dot(act.astype(jnp.bfloat16), w2b[...])
        row = lax.broadcasted_iota(jnp.int32, (y.shape[0], 1), 0)
        mine = (row >= lo_ref[q]) & (row < hi_ref[q])

        @pl.when(newtile_ref[q] == 1)
        def _():
            y_ref[...] = jnp.where(mine, y, 0.0)

        @pl.when(newtile_ref[q] == 0)
        def _():
            y_ref[...] = jnp.where(mine, y, y_ref[...])


def _experts(plan, xg, w1, w3, w2):
    p, dw = xg.shape
    n_e, d, d_e = w1.shape
    rt = ROW_TILE
    n_steps = p // rt + n_e - 1
    grid_spec = pltpu.PrefetchScalarGridSpec(
        num_scalar_prefetch=len(plan),
        grid=(n_steps,),
        in_specs=[
            pl.BlockSpec((rt, dw), lambda q, tile, *_: (tile[q], 0)),
            pl.BlockSpec(memory_space=pl.ANY),
            pl.BlockSpec(memory_space=pl.ANY),
            pl.BlockSpec(memory_space=pl.ANY),
        ],
        out_specs=pl.BlockSpec((rt, d), lambda q, tile, *_: (tile[q], 0)),
        scratch_shapes=[
            pltpu.VMEM((d, d_e), jnp.float32),
            pltpu.VMEM((d, d_e), jnp.float32),
            pltpu.VMEM((d_e, d), jnp.float32),
            pltpu.VMEM((d, d_e), jnp.bfloat16),
            pltpu.VMEM((d, d_e), jnp.bfloat16),
            pltpu.VMEM((d_e, d), jnp.bfloat16),
            pltpu.SemaphoreType.DMA((3,)),
        ],
    )
    return pl.pallas_call(
        _expert_kernel,
        grid_spec=grid_spec,
        out_shape=jax.ShapeDtypeStruct((p, d), jnp.float32),
        compiler_params=pltpu.CompilerParams(
            dimension_semantics=("arbitrary",), vmem_limit_bytes=VMEM_LIMIT),
        name="experts",
    )(*plan, xg, w1, w3, w2)


def _expert_plan(counts, n_rows):
    n_e = counts.shape[0]
    rt = ROW_TILE
    n_steps = n_rows // rt + n_e - 1
    end = jnp.cumsum(counts)
    off = end - counts
    first_tile = off // rt
    n_tiles = jnp.where(counts > 0, (end - 1) // rt - first_tile + 1, 0)
    q_end = jnp.cumsum(n_tiles)
    q_off = q_end - n_tiles
    nq = q_end[-1]
    q = jnp.minimum(jnp.arange(n_steps, dtype=jnp.int32), nq - 1)

    def expert_of(qq):
        e = jnp.sum((qq[:, None] >= q_end[None, :]).astype(jnp.int32), axis=1)
        return jnp.minimum(e, n_e - 1)

    exp = expert_of(q)
    tile = first_tile[exp] + q - q_off[exp]
    lo = jnp.maximum(off[exp] - tile * rt, 0)
    hi = jnp.minimum(end[exp] - tile * rt, rt)
    newexp = q == q_off[exp]
    newtile = jnp.concatenate([jnp.ones((1,), bool), tile[1:] != tile[:-1]])
    hasnext = q_end[exp] < nq
    nextexp = expert_of(jnp.minimum(q_end[exp], nq - 1))
    plan = (tile, exp, lo, hi, newexp, newtile, nextexp, hasnext, nq.reshape(1))
    return tuple(a.astype(jnp.int32) for a in plan), off


def _combine_kernel(pos_ref, posn_ref, x1_ref, mf_ref, gfin_ref, y_hbm, out_ref, ybuf, sem,
                    *, final_norm):
    i = pl.program_id(0)
    n = pl.num_programs(0)
    tt = x1_ref.shape[0]
    slot = i % 2

    def issue_all(p_ref, s):
        def issue(g, c):
            for u in range(ISSUE_UNROLL):
                r = g * ISSUE_UNROLL + u
                for k in range(2):
                    pltpu.make_async_copy(y_hbm.at[pl.ds(p_ref[0, 0, 2 * r + k], 1)],
                                          ybuf.at[s, k, pl.ds(r, 1)], sem.at[s]).start()
            return c
        lax.fori_loop(0, tt // ISSUE_UNROLL, issue, 0)

    @pl.when(i == 0)
    def _():
        issue_all(pos_ref, 0)

    @pl.when(i + 1 < n)
    def _():
        issue_all(posn_ref, 1 - slot)

    for k in range(2):
        pltpu.make_async_copy(y_hbm.at[pl.ds(0, tt)], ybuf.at[slot, k], sem.at[slot]).wait()

    mf = mf_ref[...]
    x2 = x1_ref[...] + mf[:, 0:1] * ybuf[slot, 0] + mf[:, 1:2] * ybuf[slot, 1]
    out_ref[...] = _rms(x2, gfin_ref[...]) if final_norm else x2


def _combine(pos, x1, mf, gfin, y, final_norm):
    t, d = x1.shape
    tt = COMBINE_TILE
    n = t // tt
    pos3 = pos.reshape(n, 1, 2 * tt)
    return pl.pallas_call(
        functools.partial(_combine_kernel, final_norm=final_norm),
        grid=(n,),
        in_specs=[
            pl.BlockSpec((1, 1, 2 * tt), lambda i: (i, 0, 0), memory_space=pltpu.SMEM),
            pl.BlockSpec((1, 1, 2 * tt), lambda i: (jnp.minimum(i + 1, n - 1), 0, 0),
                         memory_space=pltpu.SMEM),
            pl.BlockSpec((tt, d), lambda i: (i, 0)),
            pl.BlockSpec((tt, LANES), lambda i: (i, 0)),
            pl.BlockSpec((1, d), lambda i: (0, 0)),
            pl.BlockSpec(memory_space=pl.ANY),
        ],
        out_specs=pl.BlockSpec((tt, d), lambda i: (i, 0)),
        out_shape=jax.ShapeDtypeStruct((t, d), jnp.float32),
        scratch_shapes=[pltpu.VMEM((2, 2, tt, d), jnp.float32), pltpu.SemaphoreType.DMA((2,))],
        compiler_params=pltpu.CompilerParams(dimension_semantics=("arbitrary",)),
        name="combine",
    )(pos3, pos3, x1, mf, gfin, y)


def _layer(x, gmix, w_in, caw, cbw, cbb, lng, lnb, beta_a, beta_b, w_out, gffn,
           w_rg, b_rg, w_re, b_re, w1, w3, w2):
    bsz, seq, d = x.shape
    t = bsz * seq
    d_mix = w_out.shape[0]
    bf = jnp.bfloat16
    n_route = N_GROUPS + N_EXPERTS
    wr = jnp.pad(jnp.concatenate([w_rg, w_re], axis=1), ((0, 0), (0, LANES - n_route))).astype(bf)
    br = jnp.pad(jnp.concatenate([b_rg, b_re]), (0, LANES - n_route)).reshape(1, LANES)
    head_of = jnp.arange(d_mix, dtype=jnp.int32) // HEAD_CH
    hsum = (head_of[:, None] == jnp.arange(LANES, dtype=jnp.int32)[None, :]).astype(bf)
    hexp = hsum.T
    row = lambda a: a.reshape(1, -1)

    x1, h2p, meta, mf, cnt = _mixer(
        x, row(gmix), w_in.astype(bf), caw, cbw, row(cbb), row(lng), row(lnb),
        row(jnp.concatenate([beta_a, beta_b])), w_out.astype(bf), row(gffn), wr, br, hsum, hexp)

    counts = cnt[0, N_GROUPS:N_GROUPS + N_EXPERTS].astype(jnp.int32)
    plan, off = _expert_plan(counts, 2 * t)
    rec = meta.astype(jnp.int32)
    e01 = jnp.stack([rec[:, 0, :], rec[:, 1, :]], axis=-1).reshape(t, 2)
    rank01 = jnp.stack([rec[:, 2, :], rec[:, 3, :]], axis=-1).reshape(t, 2)
    onehot = e01[..., None] == jnp.arange(N_EXPERTS, dtype=jnp.int32)
    pos = jnp.sum(jnp.where(onehot, off, 0), axis=-1) + rank01

    xg = _dispatch(pos, h2p)
    y = _experts(plan, xg, w1, w3, w2)
    return x1.reshape(t, d), pos, mf, y


def kernel(x, norm_mix_g, w_in, conv_a_w, conv_b_w, conv_b_bias, ln_b_g, ln_b_b, beta_a, beta_b,
           w_out, norm_ffn_g, w_route_group, b_route_group, w_route_expert, b_route_expert,
           w1, w3, w2, norm_final_g):
    depth = w_in.shape[0]
    bsz, seq, d = x.shape
    gfin = norm_final_g.reshape(1, d)
    for l in range(depth):
        x1, pos, mf, y = _layer(
            x, norm_mix_g[l], w_in[l], conv_a_w[l], conv_b_w[l], conv_b_bias[l], ln_b_g[l],
            ln_b_b[l], beta_a[l], beta_b[l], w_out[l], norm_ffn_g[l], w_route_group[l],
            b_route_group[l], w_route_expert[l], b_route_expert[l], w1[l], w3[l], w2[l])
        out = _combine(pos, x1, mf, gfin, y, final_norm=(l == depth - 1))
        x = out.reshape(bsz, seq, d)
    return x
```

```python
import functools

import jax
import jax.numpy as jnp
from jax import lax
from jax.experimental import pallas as pl
from jax.experimental.pallas import tpu as pltpu

EPS = 1e-6
HEAD_CH = 64
N_GROUPS = 4
EXPERTS_PER_GROUP = 8
N_EXPERTS = N_GROUPS * EXPERTS_PER_GROUP

LANES = 128
SUBLANES = 8
HALO = 16
SEQ_TILE = 512
CONV_ROWS = 64
ROW_TILE = 256
GROUP = SUBLANES
TILE_GROUPS = ROW_TILE // GROUP
ISSUE_UNROLL = 8
VMEM_LIMIT = 60 * 1024 * 1024


def _region_rows(ts):
    need = 2 * ts + N_EXPERTS * (GROUP - 1) + GROUP
    return -(-need // LANES) * LANES


def _rms(x, g):
    return x * lax.rsqrt(jnp.mean(x * x, axis=-1, keepdims=True) + EPS) * g


def _split_bf16(v):
    hi = v.astype(jnp.bfloat16)
    lo = (v - hi.astype(jnp.float32)).astype(jnp.bfloat16)
    return hi, lo


def _dot(a, b):
    return jnp.dot(a, b, preferred_element_type=jnp.float32)


def _onehot_bf16(mask):
    return jnp.where(mask, 1.0, 0.0).astype(jnp.bfloat16)


_N_PIECES = 3
GATHER_SLOTS = 3


def _mixer_kernel(xm_ref, xp_ref, xn_ref, gmix_ref, win_ref, caw_ref, cbw_ref, cbb_ref,
                  lng_ref, lnb_ref, beta_ref, wout_ref, gffn_ref, wr_ref, br_ref,
                  hsum_ref, hexp_ref,
                  x1_ref, xc_ref, mf_ref, cnt_ref,
                  hbuf, v_s, u_sh, y_s, *, seq_len, d_a, d_b, ka, kb):
    ts = xm_ref.shape[1]
    tsh = ts + 2 * HALO
    n_sh = tsh - SUBLANES
    j = pl.program_id(1)

    gmix = gmix_ref[...]
    hbuf[0:HALO, :] = _rms(xp_ref[0], gmix).astype(jnp.bfloat16)
    hbuf[HALO:HALO + ts, :] = _rms(xm_ref[0], gmix).astype(jnp.bfloat16)
    hbuf[HALO + ts:tsh, :] = _rms(xn_ref[0], gmix).astype(jnp.bfloat16)

    pos = j * ts - HALO + lax.broadcasted_iota(jnp.int32, (tsh, 1), 0)
    valid = (pos >= 0) & (pos < seq_len)

    h = hbuf[...]
    a_in = _dot(h, win_ref[:, 0:d_a])
    a_c = _dot(h, win_ref[:, 2 * d_a:3 * d_a])
    v_s[...] = jnp.where(valid, a_c * a_in, 0.0)
    b_val = _dot(h, win_ref[:, 3 * d_a:3 * d_a + d_b])
    b_gate = _dot(h, win_ref[:, 3 * d_a + d_b:3 * d_a + 2 * d_b])
    u_sh[0] = jnp.where(valid, b_val * jax.nn.sigmoid(b_gate), 0.0)
    for r in range(1, SUBLANES):
        u_sh[r, 0:n_sh, :] = u_sh[0, r:r + n_sh, :]
    y_s[:, 0:d_a] = _dot(hbuf[HALO:HALO + ts, :], win_ref[:, d_a:2 * d_a])

    pad_a = (ka - 1) // 2
    pad_b = (kb - 1) // 2
    for r0 in range(0, ts, CONV_ROWS):
        acc = caw_ref[0:1, :] * v_s[HALO + r0 - pad_a:HALO + r0 - pad_a + CONV_ROWS, :]
        for k in range(1, ka):
            s = HALO + r0 + k - pad_a
            acc = acc + caw_ref[k:k + 1, :] * v_s[s:s + CONV_ROWS, :]
        y_s[r0:r0 + CONV_ROWS, 0:d_a] = y_s[r0:r0 + CONV_ROWS, 0:d_a] * acc
        acc = cbb_ref[...]
        for k in range(kb):
            s = HALO + r0 + k - pad_b
            base = s - s % SUBLANES
            acc = acc + cbw_ref[k:k + 1, :] * u_sh[s % SUBLANES, base:base + CONV_ROWS, :]
        y_s[r0:r0 + CONV_ROWS, d_a:d_a + d_b] = acc

    ub = y_s[:, d_a:d_a + d_b]
    mu = jnp.mean(ub, axis=-1, keepdims=True)
    cen = ub - mu
    var = jnp.mean(cen * cen, axis=-1, keepdims=True)
    z = cen * lax.rsqrt(var + EPS) * lng_ref[...] + lnb_ref[...]
    y_s[:, d_a:d_a + d_b] = z * jax.nn.sigmoid(z)

    y = y_s[...]
    sq_hi, sq_lo = _split_bf16(y * y)
    ssq = _dot(sq_hi, hsum_ref[...]) + _dot(sq_lo, hsum_ref[...])
    rs = lax.rsqrt(ssq * (1.0 / HEAD_CH) + EPS)
    rs_hi, rs_lo = _split_bf16(rs)
    scale = _dot(rs_hi, hexp_ref[...]) + _dot(rs_lo, hexp_ref[...])
    yn = (y * scale * beta_ref[...]).astype(jnp.bfloat16)

    x1 = xm_ref[0] + _dot(yn, wout_ref[...])
    x1_ref[0] = x1
    h2b = _rms(x1, gffn_ref[...]).astype(jnp.bfloat16)
    d = h2b.shape[1]

    logits = _dot(h2b, wr_ref[...]) + br_ref[...]
    lane = lax.broadcasted_iota(jnp.int32, (ts, LANES), 1)
    neg_inf = -jnp.inf
    gl = jnp.where(lane < N_GROUPS, logits, neg_inf)
    gmax = jnp.max(gl, axis=-1, keepdims=True)
    g_idx = jnp.min(jnp.where(gl == gmax, lane, LANES), axis=-1, keepdims=True)
    p_sel = 1.0 / jnp.sum(jnp.exp(gl - gmax), axis=-1, keepdims=True)
    e_lane = lane - N_GROUPS
    in_grp = (e_lane >= g_idx * EXPERTS_PER_GROUP) & (e_lane < (g_idx + 1) * EXPERTS_PER_GROUP)
    el = jnp.where(in_grp, logits, neg_inf)
    m0 = jnp.max(el, axis=-1, keepdims=True)
    i0 = jnp.min(jnp.where(el == m0, lane, LANES), axis=-1, keepdims=True)
    el1 = jnp.where(lane == i0, neg_inf, el)
    m1 = jnp.max(el1, axis=-1, keepdims=True)
    i1 = jnp.min(jnp.where(el1 == m1, lane, LANES), axis=-1, keepdims=True)
    ratio = jnp.exp(m1 - m0)
    c0 = p_sel / (1.0 + ratio)
    c1 = p_sel * ratio / (1.0 + ratio)

    hot0 = lane == i0
    hot1 = lane == i1
    assign = jnp.where(hot0 | hot1, 1.0, 0.0)
    tri = (lax.broadcasted_iota(jnp.int32, (ts, ts), 1)
           < lax.broadcasted_iota(jnp.int32, (ts, ts), 0))
    before = _dot(_onehot_bf16(tri), assign.astype(jnp.bfloat16))
    n_tok = jnp.sum(assign, axis=0, keepdims=True)
    n_grp = jnp.floor((n_tok + (GROUP - 1)) * (1.0 / GROUP))
    upper = (lax.broadcasted_iota(jnp.int32, (LANES, LANES), 0)
             < lax.broadcasted_iota(jnp.int32, (LANES, LANES), 1))
    grp_off = _dot(jnp.broadcast_to(n_grp, (SUBLANES, LANES)).astype(jnp.bfloat16),
                   _onehot_bf16(upper))[0:1, :]
    row_of = grp_off * GROUP + before
    p0 = jnp.sum(jnp.where(hot0, row_of, 0.0), axis=-1, keepdims=True)
    p1 = jnp.sum(jnp.where(hot1, row_of, 0.0), axis=-1, keepdims=True)
    used = jnp.sum(n_grp, axis=-1, keepdims=True)
    lane_row = lax.broadcasted_iota(jnp.int32, (1, LANES), 1)
    cnt_ref[0] = jnp.where(lane_row == 0, used, n_grp)

    rec = jnp.where(lane == 0, p0, jnp.where(lane == 1, p1, 0.0))
    mf_ref[...] = rec
    rec_t = rec.T
    n_rows = xc_ref.shape[0]
    row_id = lax.broadcasted_iota(jnp.int32, (n_rows, ts), 0).astype(jnp.float32)
    sel = (row_id == rec_t[0:1, :]) | (row_id == rec_t[1:2, :])

    def pieces(c):
        hi = c.astype(jnp.bfloat16).astype(jnp.float32)
        mid = (c - hi).astype(jnp.bfloat16).astype(jnp.float32)
        return hi, mid, (c - hi) - mid

    side = jnp.where(lane == 2 * _N_PIECES, (i0 - N_GROUPS).astype(jnp.float32), 0.0)
    for k, piece in enumerate(pieces(c0) + pieces(c1)):
        side = jnp.where(lane == k, piece, side)
    xs = _dot(_onehot_bf16(sel), jnp.concatenate([h2b, side.astype(jnp.bfloat16)], axis=1))
    xc_ref[:, 0:d // 2] = pltpu.pack_elementwise([xs[:, 0:d // 2], xs[:, d // 2:d]],
                                                 packed_dtype=jnp.bfloat16)
    side_rows = xs[:, d:d + LANES]
    xc_ref[:, d // 2:d // 2 + LANES] = pltpu.pack_elementwise(
        [side_rows, jnp.zeros_like(side_rows)], packed_dtype=jnp.bfloat16)


def _mixer(x, gmix, win, caw, cbw, cbb, lng, lnb, beta, wout, gffn, wr, br, hsum, hexp):
    bsz, seq, d = x.shape
    d_a = caw.shape[1]
    d_b = cbw.shape[1]
    ts = SEQ_TILE
    nt = seq // ts
    tsh = ts + 2 * HALO
    t = bsz * seq
    hb = ts // HALO
    rr = _region_rows(ts)
    row_words = d // 2 + LANES

    def full(a):
        return pl.BlockSpec(a.shape, lambda b, j: (0,) * a.ndim)

    kern = functools.partial(_mixer_kernel, seq_len=seq, d_a=d_a, d_b=d_b,
                             ka=caw.shape[0], kb=cbw.shape[0])
    params = (gmix, win, caw, cbw, cbb, lng, lnb, beta, wout, gffn, wr, br, hsum, hexp)
    return pl.pallas_call(
        kern,
        grid=(bsz, nt),
        in_specs=[
            pl.BlockSpec((1, ts, d), lambda b, j: (b, j, 0)),
            pl.BlockSpec((1, HALO, d), lambda b, j: (b, jnp.maximum(j * hb - 1, 0), 0)),
            pl.BlockSpec((1, HALO, d), lambda b, j: (b, jnp.minimum((j + 1) * hb, seq // HALO - 1), 0)),
        ] + [full(a) for a in params],
        out_specs=[
            pl.BlockSpec((1, ts, d), lambda b, j: (b, j, 0)),
            pl.BlockSpec((rr, row_words), lambda b, j: (b * nt + j, 0)),
            pl.BlockSpec((ts, LANES), lambda b, j: (b * nt + j, 0)),
            pl.BlockSpec((1, 1, LANES), lambda b, j: (b * nt + j, 0, 0)),
        ],
        out_shape=[
            jax.ShapeDtypeStruct((bsz, seq, d), jnp.float32),
            jax.ShapeDtypeStruct((t // ts * rr, row_words), jnp.uint32),
            jax.ShapeDtypeStruct((t, LANES), jnp.float32),
            jax.ShapeDtypeStruct((t // ts, 1, LANES), jnp.float32),
        ],
        scratch_shapes=[
            pltpu.VMEM((tsh, d), jnp.bfloat16),
            pltpu.VMEM((tsh, d_a), jnp.float32),
            pltpu.VMEM((SUBLANES, tsh, d_b), jnp.float32),
            pltpu.VMEM((ts, d_a + d_b), jnp.float32),
        ],
        compiler_params=pltpu.CompilerParams(
            dimension_semantics=("arbitrary", "arbitrary"), vmem_limit_bytes=VMEM_LIMIT),
        name="mixer_router",
    )(x, x, x, *params)


def _expert_kernel(exp_ref, toff_ref, tend_ref, src_ref,
                   xc_hbm, w1_hbm, w3_hbm, w2_hbm, y_ref,
                   xbuf, w1s, w3s, w2s, w1b, w3b, w2b, gsem, wsem):
    q = pl.program_id(0)
    n = pl.num_programs(0)
    n_slots, rt, _ = xbuf.shape
    slot = q % n_slots
    dw = w1s.shape[0] // 2
    n_tiles = tend_ref[tend_ref.shape[0] - 1]
    e = exp_ref[q]

    def gather(tile, s):
        for g in range(TILE_GROUPS):
            r = pl.multiple_of(src_ref[tile * TILE_GROUPS + g], GROUP)
            pltpu.make_async_copy(xc_hbm.at[pl.ds(r, GROUP)],
                                  xbuf.at[s, pl.ds(g * GROUP, GROUP)], gsem.at[s]).start()

    def wait_gather(s):
        pltpu.make_async_copy(xc_hbm.at[pl.ds(0, rt)], xbuf.at[s], gsem.at[s]).wait()

    def weight_copies(e):
        return (pltpu.make_async_copy(w1_hbm.at[e], w1s, wsem.at[0]),
                pltpu.make_async_copy(w3_hbm.at[e], w3s, wsem.at[1]),
                pltpu.make_async_copy(w2_hbm.at[e], w2s, wsem.at[2]))

    ahead = n_slots - 1

    @pl.when(q == 0)
    def _():
        for a in range(ahead):
            gather(a, a)
        for c in weight_copies(exp_ref[0]):
            c.start(priority=1)

    @pl.when(q < n_tiles)
    def _():
        wait_gather(slot)

        @pl.when(q == toff_ref[e])
        def _():
            for c in weight_copies(e):
                c.wait()
            w1b[...] = w1s[...].astype(jnp.bfloat16)
            w3b[...] = w3s[...].astype(jnp.bfloat16)
            w2b[...] = w2s[...].astype(jnp.bfloat16)

            @pl.when(tend_ref[e] < n_tiles)
            def _():
                for c in weight_copies(exp_ref[tend_ref[e]]):
                    c.start(priority=1)

        gather(q + ahead, (q + ahead) % n_slots)
        xw = xbuf[slot]
        xu = xw[:, 0:dw]
        x = jnp.concatenate(
            [pltpu.unpack_elementwise(xu, index=k, packed_dtype=jnp.bfloat16,
                                      unpacked_dtype=jnp.float32).astype(jnp.bfloat16)
             for k in range(2)], axis=1)
        side = pltpu.unpack_elementwise(xw[:, dw:dw + LANES], index=0,
                                        packed_dtype=jnp.bfloat16, unpacked_dtype=jnp.float32)
        c_first = (side[:, 0:1] + side[:, 1:2]) + side[:, 2:3]
        c_second = (side[:, 3:4] + side[:, 4:5]) + side[:, 5:6]
        is_first = side[:, 2 * _N_PIECES:2 * _N_PIECES + 1] == e.astype(jnp.float32)
        wgt = jnp.where(is_first, c_first, c_second)
        act = jax.nn.silu(_dot(x, w1b[...])) * _dot(x, w3b[...])
        y_ref[...] = wgt * _dot(act.astype(jnp.bfloat16), w2b[...])

    @pl.when(q >= n_tiles)
    def _():
        gather(q + ahead, (q + ahead) % n_slots)
        wait_gather(slot)
        y_ref[...] = jnp.zeros_like(y_ref)

    @pl.when(q == n - 1)
    def _():
        for a in range(1, n_slots):
            wait_gather((q + a) % n_slots)


def _experts(plan, xc, w1, w3, w2, n_steps):
    row_words = xc.shape[1]
    n_e, d, d_e = w1.shape
    rt = ROW_TILE
    grid_spec = pltpu.PrefetchScalarGridSpec(
        num_scalar_prefetch=len(plan),
        grid=(n_steps,),
        in_specs=[pl.BlockSpec(memory_space=pl.ANY)] * 4,
        out_specs=pl.BlockSpec((rt, d), lambda q, *_: (q, 0)),
        scratch_shapes=[
            pltpu.VMEM((GATHER_SLOTS, rt, row_words), jnp.uint32),
            pltpu.VMEM((d, d_e), jnp.float32),
            pltpu.VMEM((d, d_e), jnp.float32),
            pltpu.VMEM((d_e, d), jnp.float32),
            pltpu.VMEM((d, d_e), jnp.bfloat16),
            pltpu.VMEM((d, d_e), jnp.bfloat16),
            pltpu.VMEM((d_e, d), jnp.bfloat16),
            pltpu.SemaphoreType.DMA((GATHER_SLOTS,)),
            pltpu.SemaphoreType.DMA((3,)),
        ],
    )
    return pl.pallas_call(
        _expert_kernel,
        grid_spec=grid_spec,
        out_shape=jax.ShapeDtypeStruct((n_steps * rt, d), jnp.float32),
        compiler_params=pltpu.CompilerParams(
            dimension_semantics=("arbitrary",), vmem_limit_bytes=VMEM_LIMIT),
        name="experts",
    )(*plan, xc, w1, w3, w2)


def _plan(n_grp, gpc):
    n_c, n_e = n_grp.shape
    tg = TILE_GROUPS
    n_steps = n_c * gpc // tg + n_e
    i32 = jnp.int32
    off_in_chunk = jnp.cumsum(n_grp, axis=1) - n_grp
    end_in_exp = jnp.cumsum(n_grp, axis=0)
    off_in_exp = end_in_exp - n_grp
    total = end_in_exp[-1]
    tiles = (total + tg - 1) // tg
    t_end = jnp.cumsum(tiles)
    t_off = t_end - tiles
    n_tiles = t_end[-1]

    n_q = n_steps + GATHER_SLOTS - 1
    f32 = jnp.float32
    lanes = lambda a: jnp.broadcast_to(a.astype(f32)[:, :, None], a.shape + (LANES,))
    src_group = jnp.arange(n_c, dtype=i32)[:, None] * gpc + off_in_chunk
    sorted_off = t_off[None, :] * tg + off_in_exp
    src, exp, dst = pl.pallas_call(
        functools.partial(_plan_kernel, zero_group=gpc - 1),
        in_specs=[pl.BlockSpec(memory_space=pltpu.SMEM)] * 2 + [pl.BlockSpec(memory_space=pltpu.VMEM)] * 6,
        out_specs=[pl.BlockSpec(memory_space=pltpu.VMEM)] * 3,
        out_shape=[jax.ShapeDtypeStruct((n_q, 1, LANES), i32),
                   jax.ShapeDtypeStruct((n_q, 1, LANES), i32),
                   jax.ShapeDtypeStruct((n_c, 1, 2 * LANES), i32)],
        name="plan",
    )(t_off.astype(i32), t_end.astype(i32),
      lanes(off_in_exp.T), lanes(n_grp.T), lanes(src_group.T),
      lanes(off_in_chunk), lanes(n_grp), lanes(sorted_off))
    plan = (exp[:, 0, 0], t_off.astype(i32), t_end.astype(i32), src[:, 0, :tg].reshape(-1))
    return plan, dst[:, 0, :gpc].reshape(-1), n_steps


def _plan_kernel(toff_ref, tend_ref, start_ref, len_ref, srcg_ref, coff_ref, clen_ref, cpos_ref,
                 src_ref, exp_ref, dst_ref, *, zero_group):
    lane = lax.broadcasted_iota(jnp.int32, (1, LANES), 1).astype(jnp.float32)
    n_e = start_ref.shape[0]

    def pad_body(q, carry):
        src_ref[q] = jnp.full((1, LANES), zero_group * GROUP, jnp.int32)
        exp_ref[q] = jnp.full((1, LANES), n_e - 1, jnp.int32)
        return carry

    lax.fori_loop(0, src_ref.shape[0], pad_body, 0)

    def expert_body(e, carry):
        start = start_ref[e]
        length = len_ref[e]
        srcg = srcg_ref[e]

        def tile_body(q, c):
            g = ((q - toff_ref[e]) * TILE_GROUPS).astype(jnp.float32) + lane
            hit = (start <= g) & (g < start + length)
            row = jnp.sum(jnp.where(hit, srcg + (g - start), 0.0), axis=0, keepdims=True)
            n_hit = jnp.sum(jnp.where(hit, 1.0, 0.0), axis=0, keepdims=True)
            src_ref[q] = (jnp.where(n_hit > 0.0, row, float(zero_group)) * GROUP).astype(jnp.int32)
            exp_ref[q] = jnp.full((1, LANES), e, jnp.int32)
            return c

        return lax.fori_loop(toff_ref[e], tend_ref[e], tile_body, carry)

    lax.fori_loop(0, n_e, expert_body, 0)

    def chunk_body(c, carry):
        off = coff_ref[c]
        for half in range(dst_ref.shape[2] // LANES):
            j = lane + float(half * LANES)
            hit = (off <= j) & (j < off + clen_ref[c])
            row = jnp.sum(jnp.where(hit, cpos_ref[c] + (j - off), 0.0), axis=0, keepdims=True)
            dst_ref[c, :, half * LANES:(half + 1) * LANES] = (row * GROUP).astype(jnp.int32)
        return carry

    lax.fori_loop(0, dst_ref.shape[0], chunk_body, 0)


def _combine_kernel(dst_ref, used_ref, x1_ref, mf_ref, gfin_ref, ys_hbm, out_ref, ybuf, sem,
                    *, final_norm):
    c = pl.program_id(0)
    n = pl.num_programs(0)
    rr = ybuf.shape[1]
    gpc = rr // GROUP
    ts = x1_ref.shape[0]
    slot = c % 2

    def gather(chunk, s):
        def issue(i, carry):
            for u in range(ISSUE_UNROLL):
                g = i * ISSUE_UNROLL + u
                r = pl.multiple_of(dst_ref[chunk * gpc + g], GROUP)
                pltpu.make_async_copy(ys_hbm.at[pl.ds(r, GROUP)],
                                      ybuf.at[s, pl.ds(pl.multiple_of(g * GROUP, GROUP), GROUP)],
                                      sem.at[s]).start()
            return carry
        lax.fori_loop(0, gpc // ISSUE_UNROLL, issue, 0)

    @pl.when(c == 0)
    def _():
        gather(0, 0)

    @pl.when(c + 1 < n)
    def _():
        gather(c + 1, 1 - slot)

    pltpu.make_async_copy(ys_hbm.at[pl.ds(0, rr)], ybuf.at[slot], sem.at[slot]).wait()

    row = lax.broadcasted_iota(jnp.int32, (rr, 1), 0)
    yc = jnp.where(row < used_ref[c] * GROUP, ybuf[slot], 0.0)
    y_hi, y_lo = _split_bf16(yc)
    mf = mf_ref[...]
    col = lax.broadcasted_iota(jnp.int32, (ts, rr), 1).astype(jnp.float32)
    unperm = _onehot_bf16((col == mf[:, 0:1]) | (col == mf[:, 1:2]))
    x2 = x1_ref[...] + (_dot(unperm, y_hi) + _dot(unperm, y_lo))
    out_ref[...] = _rms(x2, gfin_ref[...]) if final_norm else x2


def _combine(dst, used, x1, mf, gfin, ys, rr, final_norm):
    t, d = x1.shape
    ts = SEQ_TILE
    grid_spec = pltpu.PrefetchScalarGridSpec(
        num_scalar_prefetch=2,
        grid=(t // ts,),
        in_specs=[
            pl.BlockSpec((ts, d), lambda c, *_: (c, 0)),
            pl.BlockSpec((ts, LANES), lambda c, *_: (c, 0)),
            pl.BlockSpec((1, d), lambda c, *_: (0, 0)),
            pl.BlockSpec(memory_space=pl.ANY),
        ],
        out_specs=pl.BlockSpec((ts, d), lambda c, *_: (c, 0)),
        scratch_shapes=[pltpu.VMEM((2, rr, d), jnp.float32), pltpu.SemaphoreType.DMA((2,))],
    )
    return pl.pallas_call(
        functools.partial(_combine_kernel, final_norm=final_norm),
        grid_spec=grid_spec,
        out_shape=jax.ShapeDtypeStruct((t, d), jnp.float32),
        compiler_params=pltpu.CompilerParams(
            dimension_semantics=("arbitrary",), vmem_limit_bytes=VMEM_LIMIT),
        name="combine",
    )(dst, used, x1, mf, gfin, ys)


def _layer(x, gmix, w_in, caw, cbw, cbb, lng, lnb, beta_a, beta_b, w_out, gffn,
           w_rg, b_rg, w_re, b_re, w1, w3, w2, gfin, final_norm):
    bsz, seq, d = x.shape
    t = bsz * seq
    d_mix = w_out.shape[0]
    bf = jnp.bfloat16
    n_route = N_GROUPS + N_EXPERTS
    wr = jnp.pad(jnp.concatenate([w_rg, w_re], axis=1), ((0, 0), (0, LANES - n_route))).astype(bf)
    br = jnp.pad(jnp.concatenate([b_rg, b_re]), (0, LANES - n_route)).reshape(1, LANES)
    head_of = jnp.arange(d_mix, dtype=jnp.int32) // HEAD_CH
    hsum = (head_of[:, None] == jnp.arange(LANES, dtype=jnp.int32)[None, :]).astype(bf)
    hexp = hsum.T
    row = lambda a: a.reshape(1, -1)

    x1, xc, mf, cnt = _mixer(
        x, row(gmix), w_in.astype(bf), caw, cbw, row(cbb), row(lng), row(lnb),
        row(jnp.concatenate([beta_a, beta_b])), w_out.astype(bf), row(gffn), wr, br, hsum, hexp)

    rr = _region_rows(SEQ_TILE)
    n_grp = cnt[:, 0, N_GROUPS:N_GROUPS + N_EXPERTS].astype(jnp.int32)
    used = cnt[:, 0, 0].astype(jnp.int32)
    plan, dst, n_steps = _plan(n_grp, rr // GROUP)
    ys = _experts(plan, xc, w1, w3, w2, n_steps)
    out = _combine(dst, used, x1.reshape(t, d), mf, gfin, ys, rr, final_norm)
    return out.reshape(bsz, seq, d)


def kernel(x, norm_mix_g, w_in, conv_a_w, conv_b_w, conv_b_bias, ln_b_g, ln_b_b, beta_a, beta_b,
           w_out, norm_ffn_g, w_route_group, b_route_group, w_route_expert, b_route_expert,
           w1, w3, w2, norm_final_g):
    depth = w_in.shape[0]
    gfin = norm_final_g.reshape(1, -1)
    for l in range(depth):
        x = _layer(
            x, norm_mix_g[l], w_in[l], conv_a_w[l], conv_b_w[l], conv_b_bias[l], ln_b_g[l],
            ln_b_b[l], beta_a[l], beta_b[l], w_out[l], norm_ffn_g[l], w_route_group[l],
            b_route_group[l], w_route_expert[l], b_route_expert[l], w1[l], w3[l], w2[l],
            gfin, final_norm=(l == depth - 1))
    return x
```
